```python
import math
import jax
import jax.numpy as jnp
from jax import lax
import numpy as np

D_MODEL = 1024
BATCH = 8
SEQ = 2048
DEPTH = 1
DEC_BATCH = 8
DEC_SEQ = 32
PAST_LEN = 4096

CHUNK = 64
N_META = 16
D_LRU = 512
LRU_BLOCKS = 8
LRU_BLOCK = D_LRU // LRU_BLOCKS
CONV_W = 4
LRU_C = 8.0
MLSTM_HEADS = 4
MLSTM_HEAD_DIM = 128
D_MLSTM = MLSTM_HEADS * MLSTM_HEAD_DIM
D_MIX = D_LRU + D_MLSTM
D_IN = 2 * D_LRU + 4 * D_MLSTM + 2 * MLSTM_HEADS
PEER_HEADS = 8
PEER_KEYS = 128
PEER_EXPERTS = PEER_KEYS * PEER_KEYS
PEER_QDIM = 256
PEER_HALF = PEER_QDIM // 2
PEER_TOPK = 16
PEER_BLOCK = 128
EPS = 1e-6

kernel_name = 'hymba_rglru_mlstm_peer_stream_step'


def rmsnorm(x, g):
    xf = x.astype(jnp.float32)
    y = xf * lax.rsqrt(jnp.mean(xf * xf, axis=-1, keepdims=True) + EPS)
    return (y * g.astype(jnp.float32)).astype(x.dtype)


def causal_conv(xl, hist, w, b):
    T = xl.shape[1]
    full = jnp.concatenate([hist.astype(xl.dtype), xl], axis=1)
    out = full[:, 0:T] * w[0]
    for j in range(1, CONV_W):
        out = out + full[:, j:j + T] * w[j]
    return out + b, full[:, -(CONV_W - 1):]


def rglru(xc, wa, ba, wx, bx, lam, h0, reset_first):
    B, T, _ = xc.shape
    xb = xc.reshape(B, T, LRU_BLOCKS, LRU_BLOCK)
    r = jax.nn.sigmoid(jnp.einsum('btni,nij->btnj', xb, wa).reshape(B, T, D_LRU).astype(jnp.float32) + ba.astype(jnp.float32))
    i = jax.nn.sigmoid(jnp.einsum('btni,nij->btnj', xb, wx).reshape(B, T, D_LRU).astype(jnp.float32) + bx.astype(jnp.float32))
    log_a = -LRU_C * r * jax.nn.softplus(-lam.astype(jnp.float32))
    a = jnp.exp(log_a)
    mult = jnp.sqrt(-jnp.expm1(2.0 * log_a))
    if reset_first:
        mult = mult.at[:, 0].set(1.0)
    bt = mult * i * xc.astype(jnp.float32)
    bt = bt.at[:, 0].add(a[:, 0] * h0)

    def comb(left, right):
        al, bl = left
        ar, br = right
        return al * ar, ar * bl + br

    _, h = lax.associative_scan(comb, (a, bt), axis=1)
    return h, h[:, -1]


def mlstm_chunk(carry, q, k, v, ig, lf):
    c_prev, n_prev, m_prev = carry
    L = q.shape[1]
    b = jnp.cumsum(lf, axis=1)
    causal = jnp.tril(jnp.ones((L, L), dtype=bool))[None, :, :, None]
    d = b[:, :, None, :] - b[:, None, :, :] + ig[:, None, :, :]
    d = jnp.where(causal, d, -jnp.inf)
    inter = b + m_prev[:, None, :]
    m_tok = jnp.maximum(inter, jnp.max(d, axis=2))
    w = jnp.exp(d - m_tok[:, :, None, :])
    s = jnp.einsum('blhd,bshd->blsh', q, k) * w
    decay = jnp.exp(inter - m_tok)
    num = decay[..., None] * jnp.einsum('blhd,bhde->blhe', q, c_prev) + jnp.einsum('blsh,bshe->blhe', s, v)
    den = decay * jnp.einsum('blhd,bhd->blh', q, n_prev) + jnp.sum(s, axis=2)
    h = num / jnp.maximum(jnp.abs(den), jnp.exp(-m_tok))[..., None]
    m_new = m_tok[:, -1]
    dec_prev = jnp.exp(b[:, -1] + m_prev - m_new)
    w_end = jnp.exp(b[:, -1:, :] - b + ig - m_new[:, None, :])
    c_new = dec_prev[..., None, None] * c_prev + jnp.einsum('bsh,bshd,bshe->bhde', w_end, k, v)
    n_new = dec_prev[..., None] * n_prev + jnp.einsum('bsh,bshd->bhd', w_end, k)
    return (c_new, n_new, m_new), h


def mlstm_run(carry, q, k, v, ig, lf, lead):
    B = q.shape[0]
    outs = []
    if lead > 0:
        carry, h_lead = mlstm_chunk(carry, q[:, :lead], k[:, :lead], v[:, :lead], ig[:, :lead], lf[:, :lead])
        outs.append(h_lead)
        q, k, v, ig, lf = q[:, lead:], k[:, lead:], v[:, lead:], ig[:, lead:], lf[:, lead:]
    T = q.shape[1]
    if T <= CHUNK:
        carry, h = mlstm_chunk(carry, q, k, v, ig, lf)
    else:
        n_chunks = T // CHUNK

        def to_chunks(t):
            return jnp.moveaxis(t.reshape((B, n_chunks, CHUNK) + t.shape[2:]), 1, 0)

        def step(c, xs):
            return mlstm_chunk(c, *xs)

        carry, h = lax.scan(step, carry, (to_chunks(q), to_chunks(k), to_chunks(v), to_chunks(ig), to_chunks(lf)))
        h = jnp.moveaxis(h, 0, 1).reshape((B, T) + h.shape[3:])
    outs.append(h)
    return carry, jnp.concatenate(outs, axis=1)


def mixer(xn, conv_hist, h0, c0, n0, m0, w_in, b_gates, conv_w, conv_b, lru_wa, lru_ba, lru_wx, lru_bx,
          lru_lambda, mlstm_norm, w_out, reset_first, lead):
    B, T, _ = xn.shape
    f32 = jnp.float32
    proj = xn @ w_in
    cuts = [D_LRU, 2 * D_LRU, 2 * D_LRU + D_MLSTM, 2 * D_LRU + 2 * D_MLSTM,
            2 * D_LRU + 3 * D_MLSTM, 2 * D_LRU + 4 * D_MLSTM]
    xl, gl, q, k, v, o, gates = jnp.split(proj, cuts, axis=-1)
    xc, conv_new = causal_conv(xl, conv_hist, conv_w, conv_b)
    h_lru, h_last = rglru(xc, lru_wa, lru_ba, lru_wx, lru_bx, lru_lambda, h0.astype(f32), reset_first)
    y_lru = h_lru * jax.nn.gelu(gl.astype(f32))
    hd = (B, T, MLSTM_HEADS, MLSTM_HEAD_DIM)
    qf = q.reshape(hd).astype(f32)
    kf = k.reshape(hd).astype(f32) * (MLSTM_HEAD_DIM ** -0.5)
    vf = v.reshape(hd).astype(f32)
    g = gates.astype(f32) + b_gates.astype(f32)
    ig = g[..., :MLSTM_HEADS]
    lf = jax.nn.log_sigmoid(g[..., MLSTM_HEADS:])
    carry0 = (c0.astype(f32), n0.astype(f32), m0.astype(f32))
    (c1, n1, m1), hm = mlstm_run(carry0, qf, kf, vf, ig, lf, lead)
    hm = hm * lax.rsqrt(jnp.mean(hm * hm, axis=-1, keepdims=True) + EPS)
    y_m = hm.reshape(B, T, D_MLSTM) * mlstm_norm.astype(f32) * jax.nn.sigmoid(o.astype(f32))
    y = jnp.concatenate([y_lru, y_m], axis=-1).astype(xn.dtype) @ w_out
    return y, (conv_new, h_last, c1, n1, m1)


def peer(xn, wq, keys1, keys2, u_tab, v_tab):
    B, T, D = xn.shape
    flat = xn.reshape(B * T, D)
    n_tok = B * T
    n_blk = -(-n_tok // PEER_BLOCK)
    flat = jnp.pad(flat, ((0, n_blk * PEER_BLOCK - n_tok), (0, 0))).reshape(n_blk, PEER_BLOCK, D)

    def block(xb):
        qh = (xb @ wq).reshape(PEER_BLOCK, PEER_HEADS, 2, PEER_HALF)
        s1 = jnp.einsum('thd,nd->thn', qh[:, :, 0], keys1).astype(jnp.float32)
        s2 = jnp.einsum('thd,nd->thn', qh[:, :, 1], keys2).astype(jnp.float32)
        v1, i1 = lax.top_k(s1, PEER_TOPK)
        v2, i2 = lax.top_k(s2, PEER_TOPK)
        cand = (v1[..., :, None] + v2[..., None, :]).reshape(PEER_BLOCK, PEER_HEADS, PEER_TOPK * PEER_TOPK)
        sc, ci = lax.top_k(cand, PEER_TOPK)
        e = (jnp.take_along_axis(i1, ci // PEER_TOPK, axis=-1) * PEER_KEYS
             + jnp.take_along_axis(i2, ci % PEER_TOPK, axis=-1))
        gate = jax.nn.softmax(sc, axis=-1)
        act = jax.nn.gelu(jnp.einsum('td,thkd->thk', xb, u_tab[e]).astype(jnp.float32)) * gate
        return jnp.einsum('thk,thkd->td', act.astype(xb.dtype), v_tab[e])

    out = lax.map(block, flat).reshape(n_blk * PEER_BLOCK, D)[:n_tok]
    return out.reshape(B, T, D)


def setup_inputs(seed: int = 0) -> dict:
    key = jax.random.key(seed)
    ks = jax.random.split(key, 32)
    f32 = jnp.float32

    def nrm(k, shape, scale):
        return jax.random.normal(k, shape, f32) * scale

    u = jax.random.uniform(ks[0], (DEPTH, D_LRU), f32, 0.9, 0.999)
    s = u ** (1.0 / LRU_C)
    lru_lambda = jnp.log(s) - jnp.log1p(-s)
    b_i = nrm(ks[1], (DEPTH, MLSTM_HEADS), 0.1)
    b_f = jnp.linspace(3.0, 6.0, MLSTM_HEADS, dtype=f32)[None, :] + nrm(ks[2], (DEPTH, MLSTM_HEADS), 0.1)
    return {
        'x_prompt': nrm(ks[3], (BATCH, SEQ, D_MODEL), 1.0),
        'x_sample': nrm(ks[4], (DEC_BATCH, DEC_SEQ, D_MODEL), 1.0),
        'state_conv': nrm(ks[5], (DEPTH, DEC_BATCH, CONV_W - 1, D_LRU), 1.0),
        'state_lru': nrm(ks[6], (DEPTH, DEC_BATCH, D_LRU), 0.5),
        'state_mlstm_c': nrm(ks[7], (DEPTH, DEC_BATCH, MLSTM_HEADS, MLSTM_HEAD_DIM, MLSTM_HEAD_DIM), 0.1),
        'state_mlstm_n': nrm(ks[8], (DEPTH, DEC_BATCH, MLSTM_HEADS, MLSTM_HEAD_DIM), 0.5),
        'state_mlstm_m': nrm(ks[9], (DEPTH, DEC_BATCH, MLSTM_HEADS), 1.0),
        'meta_tokens': nrm(ks[10], (N_META, D_MODEL), 1.0),
        'w_in': nrm(ks[11], (DEPTH, D_MODEL, D_IN), D_MODEL ** -0.5),
        'b_gates': jnp.concatenate([b_i, b_f], axis=-1),
        'conv_w': nrm(ks[12], (DEPTH, CONV_W, D_LRU), CONV_W ** -0.5),
        'conv_b': nrm(ks[13], (DEPTH, D_LRU), 0.02),
        'lru_wa': nrm(ks[14], (DEPTH, LRU_BLOCKS, LRU_BLOCK, LRU_BLOCK), LRU_BLOCK ** -0.5),
        'lru_ba': nrm(ks[15], (DEPTH, D_LRU), 0.02),
        'lru_wx': nrm(ks[16], (DEPTH, LRU_BLOCKS, LRU_BLOCK, LRU_BLOCK), LRU_BLOCK ** -0.5),
        'lru_bx': nrm(ks[17], (DEPTH, D_LRU), 0.02),
        'lru_lambda': lru_lambda,
        'mlstm_norm': 1.0 + nrm(ks[18], (DEPTH, D_MLSTM), 0.01),
        'w_out': nrm(ks[19], (DEPTH, D_MIX, D_MODEL), D_MIX ** -0.5),
        'norm_mix': 1.0 + nrm(ks[20], (DEPTH, D_MODEL), 0.01),
        'norm_ffn': 1.0 + nrm(ks[21], (DEPTH, D_MODEL), 0.01),
        'peer_wq': nrm(ks[22], (DEPTH, D_MODEL, PEER_HEADS * PEER_QDIM), D_MODEL ** -0.5),
        'peer_keys1': nrm(ks[23], (DEPTH, PEER_KEYS, PEER_HALF), PEER_HALF ** -0.5),
        'peer_keys2': nrm(ks[24], (DEPTH, PEER_KEYS, PEER_HALF), PEER_HALF ** -0.5),
        'peer_u': nrm(ks[25], (DEPTH, PEER_EXPERTS, D_MODEL), D_MODEL ** -0.5),
        'peer_v': nrm(ks[26], (DEPTH, PEER_EXPERTS, D_MODEL), PEER_HEADS ** -0.5),
        'norm_final': 1.0 + nrm(ks[27], (D_MODEL,), 0.01),
    }


def reference(x_prompt, x_sample, state_conv, state_lru, state_mlstm_c, state_mlstm_n, state_mlstm_m,
              meta_tokens, w_in, b_gates, conv_w, conv_b, lru_wa, lru_ba, lru_wx, lru_bx, lru_lambda,
              mlstm_norm, w_out, norm_mix, norm_ffn, peer_wq, peer_keys1, peer_keys2, peer_u, peer_v,
              norm_final):
    f32 = jnp.float32
    B = x_prompt.shape[0]
    meta = jnp.broadcast_to(meta_tokens[None].astype(x_prompt.dtype), (B, N_META, D_MODEL))
    hp = jnp.concatenate([meta, x_prompt], axis=1)
    hs = x_sample
    zero_conv = jnp.zeros((B, CONV_W - 1, D_LRU), x_prompt.dtype)
    zero_h = jnp.zeros((B, D_LRU), f32)
    zero_c = jnp.zeros((B, MLSTM_HEADS, MLSTM_HEAD_DIM, MLSTM_HEAD_DIM), f32)
    zero_n = jnp.zeros((B, MLSTM_HEADS, MLSTM_HEAD_DIM), f32)
    zero_m = jnp.zeros((B, MLSTM_HEADS), f32)
    p_states = []
    s_states = []
    for l in range(DEPTH):
        lw = (w_in[l], b_gates[l], conv_w[l], conv_b[l], lru_wa[l], lru_ba[l], lru_wx[l], lru_bx[l],
              lru_lambda[l], mlstm_norm[l], w_out[l])
        pw = (peer_wq[l], peer_keys1[l], peer_keys2[l], peer_u[l], peer_v[l])
        yp, stp = mixer(rmsnorm(hp, norm_mix[l]), zero_conv, zero_h, zero_c, zero_n, zero_m, *lw,
                        reset_first=True, lead=N_META)
        hp = hp + yp
        hp = hp + peer(rmsnorm(hp, norm_ffn[l]), *pw)
        ys, sts = mixer(rmsnorm(hs, norm_mix[l]), state_conv[l], state_lru[l], state_mlstm_c[l],
                        state_mlstm_n[l], state_mlstm_m[l], *lw, reset_first=False, lead=0)
        hs = hs + ys
        hs = hs + peer(rmsnorm(hs, norm_ffn[l]), *pw)
        p_states.append(stp)
        s_states.append(sts)
    y_prompt = rmsnorm(hp[:, N_META:], norm_final)
    y_sample = rmsnorm(hs, norm_final)
    conv_p = jnp.stack([st[0] for st in p_states])
    lru_p = jnp.stack([st[1] for st in p_states])
    c_p = jnp.stack([st[2] for st in p_states])
    n_p = jnp.stack([st[3] for st in p_states])
    m_p = jnp.stack([st[4] for st in p_states])
    conv_s = jnp.stack([st[0] for st in s_states])
    lru_s = jnp.stack([st[1] for st in s_states])
    c_s = jnp.stack([st[2] for st in s_states])
    n_s = jnp.stack([st[3] for st in s_states])
    m_s = jnp.stack([st[4] for st in s_states])
    return (y_prompt, y_sample, conv_p, lru_p, c_p, n_p, m_p, conv_s, lru_s, c_s, n_s, m_s)
```

```python
import functools
import math

import jax
import jax.numpy as jnp
from jax import lax
from jax.experimental import pallas as pl
from jax.experimental.pallas import tpu as pltpu

F32 = jnp.float32
BF16 = jnp.bfloat16

D_MODEL = 1024
N_META = 16
D_LRU = 512
LRU_BLOCKS = 8
CONV_W = 4
LRU_C = 8.0
HEADS = 4
HEAD_DIM = 128
D_MLSTM = HEADS * HEAD_DIM
D_MAIN = 2 * D_LRU + 4 * D_MLSTM
PEER_HEADS = 8
PEER_KEYS = 128
PEER_HALF = 128
PEER_TOPK = 16
EPS = 1e-6
CHUNK = 64
LANES = 128
SUBLANES = 8
NEG = -1e30
VMEM_LIMIT = 62 * 1024 * 1024


def _split3(x):
    hi = x.astype(BF16)
    r = x - hi.astype(F32)
    mid = r.astype(BF16)
    lo = (r - mid.astype(F32)).astype(BF16)
    return hi, mid, lo


def _rmsnorm(x, g):
    return x * lax.rsqrt(jnp.mean(x * x, axis=-1, keepdims=True) + EPS) * g


def _softplus(x):
    return jnp.maximum(x, 0.0) + jnp.log1p(jnp.exp(-jnp.abs(x)))


def _gelu(x):
    return 0.5 * x * (1.0 + jnp.tanh(math.sqrt(2.0 / math.pi) * (x + 0.044715 * (x * x * x))))


def _dot(a, b):
    return jnp.dot(a, b, preferred_element_type=F32)


def _const_spec(shape):
    zeros = (0,) * len(shape)
    return pl.BlockSpec(shape, lambda *_: zeros, pipeline_mode=pl.Buffered(1))


def _dot_nt(a, b):
    return lax.dot_general(a, b, (((1,), (1,)), ((), ())), preferred_element_type=F32)


def _dot_tn(a, b):
    return lax.dot_general(a, b, (((0,), (0,)), ((), ())), preferred_element_type=F32)


def _in_proj_kernel(n_a, ha_ref, ht_ref, gmix_ref, wmain_ref, wgh_ref, wgm_ref, bg_ref,
                    xga_ref, xgt_ref, qa_ref, qt_ref, ga_ref, gt_ref):
    i = pl.program_id(0)
    h = jnp.where(i < n_a, ha_ref[...], ht_ref[...])
    z = _rmsnorm(h, gmix_ref[...])
    zh = z.astype(BF16)
    zm = (z - zh.astype(F32)).astype(BF16)
    proj = _dot(zh, wmain_ref[...])
    g = _dot(zh, wgh_ref[...]) + _dot(zh, wgm_ref[...]) + _dot(zm, wgh_ref[...]) + bg_ref[...]
    lane = lax.broadcasted_iota(jnp.int32, g.shape, 1)
    g = jnp.where(lane >= HEADS, -_softplus(-g), g)
    xg = proj[:, :2 * D_LRU]
    qkvo = proj[:, 2 * D_LRU:].astype(BF16)

    @pl.when(i < n_a)
    def _():
        xga_ref[...] = xg
        qa_ref[...] = qkvo
        ga_ref[...] = g

    @pl.when(i >= n_a)
    def _():
        xgt_ref[...] = xg
        qt_ref[...] = qkvo
        gt_ref[...] = g


def _in_proj(h_a, h_t, gmix, wmain, wgh, wgm, bg, tm):
    n_a = h_a.shape[0] // tm
    n_t = h_t.shape[0] // tm
    a_map = lambda i: (jnp.minimum(i, n_a - 1), 0)
    t_map = lambda i: (jnp.maximum(i - n_a, 0), 0)

    def pair(width, dtype):
        return ([jax.ShapeDtypeStruct((h_a.shape[0], width), dtype),
                 jax.ShapeDtypeStruct((h_t.shape[0], width), dtype)],
                [pl.BlockSpec((tm, width), a_map), pl.BlockSpec((tm, width), t_map)])

    shapes, specs = [], []
    for width, dtype in ((2 * D_LRU, F32), (4 * D_MLSTM, BF16), (LANES, F32)):
        s, p = pair(width, dtype)
        shapes += s
        specs += p
    return pl.pallas_call(
        functools.partial(_in_proj_kernel, n_a),
        grid=(n_a + n_t,),
        in_specs=[pl.BlockSpec((tm, D_MODEL), a_map), pl.BlockSpec((tm, D_MODEL), t_map),
                  _const_spec((1, D_MODEL)), _const_spec((D_MODEL, D_MAIN)),
                  _const_spec((D_MODEL, LANES)), _const_spec((D_MODEL, LANES)),
                  _const_spec((1, LANES))],
        out_specs=specs,
        out_shape=shapes,
        compiler_params=pltpu.CompilerParams(dimension_semantics=("arbitrary",),
                                             vmem_limit_bytes=VMEM_LIMIT),
        name="in_proj",
    )(h_a, h_t, gmix, wmain, wgh, wgm, bg)


def _seq_mix_kernel(nb, n_x, ts,
                    xga_ref, xgt_ref, qa_ref, qt_ref, ga_ref, gt_ref,
                    sconv_ref, slru_ref, sc_ref, sn_ref, sm_ref,
                    convw_ref, convb_ref, wa_ref, ba_ref, wx_ref, bx_ref, lam_ref, mnorm_ref,
                    ya_ref, yt_ref,
                    convp_ref, lrup_ref, cp_ref, np_ref, mp_ref,
                    convs_ref, lrus_ref, cs_ref, ns_ref, ms_ref,
                    ext_s, a_s, b_s, h_s, tail_s, hcar_s, c_s, n_s, m_s):
    c = pl.program_id(0)
    last = n_x + 1
    is_meta = c == 0
    is_sample = c == last
    on_t = jnp.logical_or(is_meta, is_sample)
    n_valid = jnp.where(is_meta, N_META, jnp.where(is_sample, ts, CHUNK))

    @pl.when(is_meta)
    def _():
        tail_s[...] = jnp.zeros_like(tail_s)
        hcar_s[...] = jnp.zeros_like(hcar_s)
        c_s[...] = jnp.zeros_like(c_s)
        n_s[...] = jnp.zeros_like(n_s)
        m_s[...] = jnp.zeros_like(m_s)

    @pl.when(is_sample)
    def _():
        tail_s[...] = sconv_ref[...]
        hcar_s[...] = slru_ref[...]
        c_s[...] = sc_ref[...]
        n_s[...] = sn_ref[...]
        m_s[...] = sm_ref[...]

    xg = jnp.where(on_t, xgt_ref[...], xga_ref[...])
    xl = xg[:, :, :D_LRU]
    gl = xg[:, :, D_LRU:]
    rows3 = lax.broadcasted_iota(jnp.int32, (nb, CHUNK, D_LRU), 1)
    valid3 = rows3 < n_valid

    ext_s[:, 0:SUBLANES, :] = tail_s[...]
    ext_s[:, SUBLANES:, :] = xl
    xc = convb_ref[...][None]
    for j in range(CONV_W):
        off = SUBLANES - (CONV_W - 1) + j
        xc = xc + ext_s[:, off:off + CHUNK, :] * convw_ref[j:j + 1, :][None]
    xc2 = xc.reshape(nb * CHUNK, D_LRU)
    xcb = xc2.astype(BF16)
    r = jax.nn.sigmoid(_dot(xcb, wa_ref[...]) + ba_ref[...])
    ig = jax.nn.sigmoid(_dot(xcb, wx_ref[...]) + bx_ref[...])
    log_a = (-LRU_C) * r * _softplus(-lam_ref[...])
    a = jnp.exp(log_a)
    th = jnp.tanh(log_a)
    mult = jnp.sqrt(-2.0 * th / (1.0 - th))
    rows2 = rows3.reshape(nb * CHUNK, D_LRU)
    mult = jnp.where(jnp.logical_and(is_meta, rows2 == 0), 1.0, mult)
    bt = mult * ig * xc2
    valid2 = valid3.reshape(nb * CHUNK, D_LRU)
    a_s[...] = jnp.where(valid2, a, 1.0).reshape(nb, CHUNK, D_LRU)
    b_s[...] = jnp.where(valid2, bt, 0.0).reshape(nb, CHUNK, D_LRU)

    for nv, cond in ((N_META, is_meta), (ts, is_sample), (CHUNK, jnp.logical_not(on_t))):
        @pl.when(cond)
        def _(nv=nv):
            tail_s[:, SUBLANES - (CONV_W - 1):, :] = ext_s[:, SUBLANES + nv - (CONV_W - 1):SUBLANES + nv, :]

    ri = lax.broadcasted_iota(jnp.int32, (CHUNK, LANES), 0)
    ci = lax.broadcasted_iota(jnp.int32, (CHUNK, LANES), 1)
    causal = ci <= ri
    ltri = (lax.broadcasted_iota(jnp.int32, (CHUNK, CHUNK), 1)
            <= lax.broadcasted_iota(jnp.int32, (CHUNK, CHUNK), 0)).astype(BF16)
    eye = (lax.broadcasted_iota(jnp.int32, (LANES, LANES), 0)
           == lax.broadcasted_iota(jnp.int32, (LANES, LANES), 1)).astype(BF16)
    row8 = lax.broadcasted_iota(jnp.int32, (SUBLANES, D_LRU), 0)
    validg = ri < n_valid
    lane_g = ci
    kscale = HEAD_DIM ** -0.5
    zpad = jnp.zeros((LANES - CHUNK, HEAD_DIM), BF16)

    def per_batch(b, carry):
        hc = hcar_s[b]
        for g8 in range(CHUNK // SUBLANES):
            sl = slice(g8 * SUBLANES, (g8 + 1) * SUBLANES)
            av = a_s[b, sl, :]
            bv = b_s[b, sl, :]
            for d in (1, 2, 4):
                keep = row8 >= d
                a_sh = pltpu.roll(av, d, 0)
                b_sh = pltpu.roll(bv, d, 0)
                bv = jnp.where(keep, av * b_sh + bv, bv)
                av = jnp.where(keep, av * a_sh, av)
            hv = av * hc + bv
            h_s[b, sl, :] = hv
            hc = hv[SUBLANES - 1:SUBLANES, :]
        hcar_s[b] = hc

        gq = jnp.where(on_t, gt_ref[b], ga_ref[b])
        gq = jnp.where(validg, gq, jnp.where(lane_g < HEADS, NEG, 0.0))
        p1, p2, p3 = _split3(gq)
        cum = _dot(ltri, p1) + _dot(ltri, p2) + _dot(ltri, p3)
        bsum = pltpu.roll(cum, LANES - HEADS, 1)
        xq = gq - bsum
        xpad = jnp.concatenate([xq, jnp.zeros((LANES - CHUNK, LANES), F32)], axis=0)
        x1, x2, x3 = _split3(xpad)
        xt = _dot_nt(eye, x1) + _dot_nt(eye, x2) + _dot_nt(eye, x3)

        qkvo = jnp.where(on_t, qt_ref[b], qa_ref[b])
        for hd in range(HEADS):
            sl = slice(hd * HEAD_DIM, (hd + 1) * HEAD_DIM)
            q = qkvo[:, sl]
            k = qkvo[:, D_MLSTM + hd * HEAD_DIM:D_MLSTM + (hd + 1) * HEAD_DIM]
            v = qkvo[:, 2 * D_MLSTM + hd * HEAD_DIM:2 * D_MLSTM + (hd + 1) * HEAD_DIM]
            o = qkvo[:, 3 * D_MLSTM + hd * HEAD_DIM:3 * D_MLSTM + (hd + 1) * HEAD_DIM]
            idx = b * HEADS + hd
            c_prev = c_s[idx]
            n_prev = n_s[idx]
            m_prev = m_s[idx]
            bc = jnp.broadcast_to(bsum[:, hd:hd + 1], (CHUNK, LANES))
            xc_col = jnp.broadcast_to(xq[:, hd:hd + 1], (CHUNK, LANES))
            xr = jnp.broadcast_to(xt[hd:hd + 1, :], (CHUNK, LANES))
            d = jnp.where(causal, bc + xr, NEG)
            mloc = jnp.broadcast_to(jnp.max(d, axis=1, keepdims=True), (CHUNK, LANES))
            inter = bc + m_prev
            m_tok = jnp.maximum(inter, mloc)
            w = jnp.exp(d - m_tok)
            decay = jnp.exp(inter - m_tok)
            kpad = jnp.concatenate([k, zpad], axis=0)
            vpad = jnp.concatenate([v, zpad], axis=0)
            s = _dot_nt(q, kpad) * (w * kscale)
            num = decay * _dot(q, c_prev.astype(BF16)) + _dot(s.astype(BF16), vpad)
            qn = jnp.sum(q.astype(F32) * n_prev, axis=1, keepdims=True)
            den = decay * qn + jnp.sum(s, axis=1, keepdims=True)
            hh = num / jnp.maximum(jnp.abs(den), jnp.exp(-m_tok))
            hm = hh * lax.rsqrt(jnp.mean(hh * hh, axis=1, keepdims=True) + EPS)
            ym = hm * mnorm_ref[:, sl] * jax.nn.sigmoid(o.astype(F32))
            ymix = ym.astype(BF16)

            @pl.when(on_t)
            def _(ymix=ymix, hd=hd):
                yt_ref[b, :, D_LRU + hd * HEAD_DIM:D_LRU + (hd + 1) * HEAD_DIM] = ymix

            @pl.when(jnp.logical_not(on_t))
            def _(ymix=ymix, hd=hd):
                ya_ref[b, :, D_LRU + hd * HEAD_DIM:D_LRU + (hd + 1) * HEAD_DIM] = ymix

            m_new = m_tok[CHUNK - 1:CHUNK, :]
            b_last = bc[CHUNK - 1:CHUNK, :]
            dec_prev = jnp.exp(b_last + m_prev - m_new)
            w_end = jnp.exp((b_last - m_new) + xc_col)
            kw = k.astype(F32) * (w_end * kscale)
            c_s[idx] = dec_prev * c_prev + _dot_tn(kw.astype(BF16), v)
            n_s[idx] = dec_prev * n_prev + jnp.sum(kw, axis=0, keepdims=True)
            m_s[idx] = m_new
        return carry

    lax.fori_loop(0, nb, per_batch, 0)

    ylru = (h_s[...] * _gelu(gl)).astype(BF16)

    @pl.when(on_t)
    def _():
        yt_ref[:, :, :D_LRU] = ylru

    @pl.when(jnp.logical_not(on_t))
    def _():
        ya_ref[:, :, :D_LRU] = ylru

    @pl.when(c == n_x)
    def _():
        convp_ref[...] = tail_s[...]
        lrup_ref[...] = hcar_s[...]
        cp_ref[...] = c_s[...]
        np_ref[...] = n_s[...]
        mp_ref[...] = m_s[...]

    @pl.when(is_sample)
    def _():
        convs_ref[...] = tail_s[...]
        lrus_ref[...] = hcar_s[...]
        cs_ref[...] = c_s[...]
        ns_ref[...] = n_s[...]
        ms_ref[...] = m_s[...]


def _seq_mix(xg_a, xg_t, q_a, q_t, g_a, g_t, states, weights, nb, n_x, ts):
    last = n_x + 1
    a_map = lambda c: (0, jnp.clip(c - 1, 0, n_x - 1), 0)
    t_map = lambda c: (0, jnp.where(c == last, 1, 0), 0)

    def full(shape):
        return pl.BlockSpec(shape, lambda c: (0,) * len(shape))

    def blk(width, m):
        return pl.BlockSpec((nb, CHUNK, width), m)

    st_shapes = [(nb, SUBLANES, D_LRU), (nb, 1, D_LRU), (nb * HEADS, HEAD_DIM, HEAD_DIM),
                 (nb * HEADS, 1, HEAD_DIM), (nb * HEADS, 1, LANES)]
    w_shapes = [w.shape for w in weights]
    in_specs = ([blk(2 * D_LRU, a_map), blk(2 * D_LRU, t_map), blk(4 * D_MLSTM, a_map),
                 blk(4 * D_MLSTM, t_map), blk(LANES, a_map), blk(LANES, t_map)]
                + [_const_spec(s) for s in st_shapes] + [_const_spec(s) for s in w_shapes])
    out_shape = ([jax.ShapeDtypeStruct((nb, n_x * CHUNK, D_MODEL), BF16),
                  jax.ShapeDtypeStruct((nb, 2 * CHUNK, D_MODEL), BF16)]
                 + [jax.ShapeDtypeStruct(s, F32) for s in st_shapes] * 2)
    out_specs = ([blk(D_MODEL, a_map), blk(D_MODEL, t_map)] + [full(s) for s in st_shapes] * 2)
    scratch = [pltpu.VMEM((nb, CHUNK + SUBLANES, D_LRU), F32),
               pltpu.VMEM((nb, CHUNK, D_LRU), F32),
               pltpu.VMEM((nb, CHUNK, D_LRU), F32),
               pltpu.VMEM((nb, CHUNK, D_LRU), F32),
               pltpu.VMEM(st_shapes[0], F32), pltpu.VMEM(st_shapes[1], F32),
               pltpu.VMEM(st_shapes[2], F32), pltpu.VMEM(st_shapes[3], F32),
               pltpu.VMEM(st_shapes[4], F32)]
    return pl.pallas_call(
        functools.partial(_seq_mix_kernel, nb, n_x, ts),
        grid=(n_x + 2,),
        in_specs=in_specs,
        out_specs=out_specs,
        out_shape=out_shape,
        scratch_shapes=scratch,
        compiler_params=pltpu.CompilerParams(dimension_semantics=("arbitrary",),
                                             vmem_limit_bytes=VMEM_LIMIT),
        name="seq_mix",
    )(xg_a, xg_t, q_a, q_t, g_a, g_t, *states, *weights)


def _cmpx(vals, i, j):
    hi = jnp.maximum(vals[i], vals[j])
    lo = jnp.minimum(vals[i], vals[j])
    vals[i], vals[j] = hi, lo


def _bitonic_clean(vals):
    n = len(vals)
    step = n // 2
    while step >= 1:
        for i in range(n):
            if (i // step) % 2 == 0:
                _cmpx(vals, i, i + step)
        step //= 2
    return vals


def _sort16(vals):
    n = len(vals)
    size = 2
    while size <= n:
        step = size // 2
        while step >= 1:
            for i in range(n):
                j = i ^ step
                if j > i:
                    if (i & size) == 0:
                        _cmpx(vals, i, j)
                    else:
                        _cmpx(vals, j, i)
            step //= 2
        size *= 2
    return vals


def _merge_top(a, b):
    n = len(a)
    top = [jnp.maximum(a[i], b[n - 1 - i]) for i in range(n)]
    return _bitonic_clean(top)


def _top16_of(ref, lt):
    lists = []
    for grp in range(PEER_KEYS // PEER_TOPK):
        vals = [ref[lt, (grp * PEER_TOPK + j) * SUBLANES:(grp * PEER_TOPK + j + 1) * SUBLANES, :]
                for j in range(PEER_TOPK)]
        lists.append(_sort16(vals))
    while len(lists) > 1:
        lists = [_merge_top(lists[i], lists[i + 1]) for i in range(0, len(lists), 2)]
    return lists[0]


def _top16_pair_sums(v1, v2):
    neg = jnp.full_like(v1[0], NEG)
    lists = []
    for a in range(PEER_TOPK):
        nbq = PEER_TOPK // (a + 1)
        col = [v1[a] + v2[bq] for bq in range(nbq)] + [neg] * (PEER_TOPK - nbq)
        lists.append(col)
    while len(lists) > 1:
        lists = [_merge_top(lists[i], lists[i + 1]) for i in range(0, len(lists), 2)]
    return lists[0]


def _peer_kernel(n_a, tm, te,
                 ha_ref, ht_ref, ya_ref, yt_ref, wout_ref, gffn_ref, wqt_ref, k1_ref, k2_ref,
                 u_ref, v_ref, gfin_ref,
                 oa_ref, ot_ref,
                 xn_s, h1_s, r1_s, s2_s, sel1_s, sel2_s, lse_s, taub_s, rb_s, ht_s, at_s, acc_s):
    i = pl.program_id(0)
    e = pl.program_id(1)
    n_e = pl.num_programs(1)
    n_lt = tm // LANES
    cpt = te // PEER_KEYS

    @pl.when(e == 0)
    def _():
        on_a = i < n_a
        h0 = jnp.where(on_a, ha_ref[...], ht_ref[...])
        ym = jnp.where(on_a, ya_ref[...], yt_ref[...])
        h1 = h0 + _dot(ym, wout_ref[...])
        h1_s[...] = h1
        xn = _rmsnorm(h1, gffn_ref[...]).astype(BF16)
        xn_s[...] = xn
        acc_s[...] = jnp.zeros_like(acc_s)
        qt = _dot_nt(wqt_ref[...], xn)
        for hd in range(PEER_HEADS):
            base = hd * 2 * PEER_HALF
            q1 = qt[base:base + PEER_HALF, :].astype(BF16)
            q2 = qt[base + PEER_HALF:base + 2 * PEER_HALF, :].astype(BF16)
            s1 = _dot(k1_ref[...], q1)
            s2 = _dot(k2_ref[...], q2)
            r1_s[hd * PEER_KEYS:(hd + 1) * PEER_KEYS, :] = s1
            s2_s[hd * PEER_KEYS:(hd + 1) * PEER_KEYS, :] = s2
            for lt in range(n_lt):
                lanes = slice(lt * LANES, (lt + 1) * LANES)
                sel1_s[lt, pl.ds(hd, PEER_KEYS, stride=SUBLANES), :] = s1[:, lanes]
                sel2_s[lt, pl.ds(hd, PEER_KEYS, stride=SUBLANES), :] = s2[:, lanes]

        def select(lt, carry):
            v1 = _top16_of(sel1_s, lt)
            v2 = _top16_of(sel2_s, lt)
            top = _top16_pair_sums(v1, v2)
            zsum = jnp.zeros_like(top[0])
            for t in top:
                zsum = zsum + jnp.exp(t - top[0])
            lse = top[0] + jnp.log(zsum)
            lse_s[lt] = lse
            top2 = _top16_pair_sums([x - lse for x in v1], v2)
            tau = top2[PEER_TOPK - 1]
            for hd in range(PEER_HEADS):
                taub_s[lt * PEER_HEADS + hd] = jnp.broadcast_to(tau[hd:hd + 1, :], (SUBLANES, LANES))
            return carry

        lax.fori_loop(0, n_lt, select, 0)
        for lt in range(n_lt):
            lanes = slice(lt * LANES, (lt + 1) * LANES)
            for hd in range(PEER_HEADS):
                rows = slice(hd * PEER_KEYS, (hd + 1) * PEER_KEYS)
                r1_s[rows, lanes] = r1_s[rows, lanes] - lse_s[lt, hd:hd + 1, :]

    ht_s[...] = _dot_nt(u_ref[...], xn_s[...])

    for hd in range(PEER_HEADS):
        rows8 = r1_s[pl.ds(pl.multiple_of(hd * PEER_KEYS + e * cpt, SUBLANES), SUBLANES), :]
        for cc in range(cpt):
            rb_s[cc * PEER_HEADS + hd] = jnp.broadcast_to(rows8[cc:cc + 1, :], (SUBLANES, tm))

    grp = PEER_KEYS // SUBLANES

    def per_key(cc, carry):
        r0 = pl.multiple_of(cc * PEER_KEYS, PEER_KEYS)
        for lt in range(n_lt):
            lanes = slice(lt * LANES, (lt + 1) * LANES)
            g = jnp.zeros((grp, SUBLANES, LANES), F32)
            for hd in range(PEER_HEADS):
                r1 = rb_s[cc * PEER_HEADS + hd][:, lanes]
                s2 = s2_s[hd * PEER_KEYS:(hd + 1) * PEER_KEYS, lanes]
                sig = s2.reshape(grp, SUBLANES, LANES) + r1[None]
                g = g + jnp.where(sig >= taub_s[lt * PEER_HEADS + hd][None], jnp.exp(sig), 0.0)
            pre = ht_s[pl.ds(r0, PEER_KEYS), lanes].reshape(grp, SUBLANES, LANES)
            act = (_gelu(pre) * g).reshape(PEER_KEYS, LANES)
            at_s[pl.ds(r0, PEER_KEYS), lanes] = act.astype(BF16)
        return carry

    lax.fori_loop(0, cpt, per_key, 0)
    acc_s[...] += _dot_tn(at_s[...], v_ref[...])

    @pl.when(e == n_e - 1)
    def _():
        y = _rmsnorm(h1_s[...] + acc_s[...], gfin_ref[...])

        @pl.when(i < n_a)
        def _():
            oa_ref[...] = y

        @pl.when(i >= n_a)
        def _():
            ot_ref[...] = y


def _peer_out(h_a, h_t, y_a, y_t, wout, gffn, wqt, k1, k2, u, v, gfin, tm, te):
    n_a = h_a.shape[0] // tm
    n_t = h_t.shape[0] // tm
    n_e = u.shape[0] // te
    a_map = lambda i, e: (jnp.minimum(i, n_a - 1), 0)
    t_map = lambda i, e: (jnp.maximum(i - n_a, 0), 0)
    n_lt = tm // LANES
    n_q = 2 * PEER_HEADS * PEER_HALF
    scratch = [pltpu.VMEM((tm, D_MODEL), BF16),
               pltpu.VMEM((tm, D_MODEL), F32),
               pltpu.VMEM((PEER_HEADS * PEER_KEYS, tm), F32),
               pltpu.VMEM((PEER_HEADS * PEER_KEYS, tm), F32),
               pltpu.VMEM((n_lt, PEER_KEYS * SUBLANES, LANES), F32),
               pltpu.VMEM((n_lt, PEER_KEYS * SUBLANES, LANES), F32),
               pltpu.VMEM((n_lt, SUBLANES, LANES), F32),
               pltpu.VMEM((n_lt * PEER_HEADS, SUBLANES, LANES), F32),
               pltpu.VMEM((te // PEER_KEYS * PEER_HEADS, SUBLANES, tm), F32),
               pltpu.VMEM((te, tm), F32),
               pltpu.VMEM((te, tm), BF16),
               pltpu.VMEM((tm, D_MODEL), F32)]
    return pl.pallas_call(
        functools.partial(_peer_kernel, n_a, tm, te),
        grid=(n_a + n_t, n_e),
        in_specs=[pl.BlockSpec((tm, D_MODEL), a_map), pl.BlockSpec((tm, D_MODEL), t_map),
                  pl.BlockSpec((tm, D_MODEL), a_map), pl.BlockSpec((tm, D_MODEL), t_map),
                  _const_spec((D_MODEL, D_MODEL)), _const_spec((1, D_MODEL)),
                  _const_spec((n_q, D_MODEL)),
                  _const_spec((PEER_KEYS, PEER_HALF)), _const_spec((PEER_KEYS, PEER_HALF)),
                  pl.BlockSpec((te, D_MODEL), lambda i, e: (e, 0)),
                  pl.BlockSpec((te, D_MODEL), lambda i, e: (e, 0)),
                  _const_spec((1, D_MODEL))],
        out_specs=[pl.BlockSpec((tm, D_MODEL), a_map), pl.BlockSpec((tm, D_MODEL), t_map)],
        out_shape=[jax.ShapeDtypeStruct(h_a.shape, F32), jax.ShapeDtypeStruct(h_t.shape, F32)],
        scratch_shapes=scratch,
        compiler_params=pltpu.CompilerParams(dimension_semantics=("arbitrary", "arbitrary"),
                                             vmem_limit_bytes=VMEM_LIMIT),
        name="peer_out",
    )(h_a, h_t, y_a, y_t, wout, gffn, wqt, k1, k2, u, v, gfin)


def _block_diag(w):
    nblk, n, _ = w.shape
    eye = jnp.eye(nblk, dtype=w.dtype)
    return (eye[:, None, :, None] * w[:, :, None, :]).reshape(nblk * n, nblk * n)


def kernel(x_prompt, x_sample, state_conv, state_lru, state_mlstm_c, state_mlstm_n, state_mlstm_m,
           meta_tokens, w_in, b_gates, conv_w, conv_b, lru_wa, lru_ba, lru_wx, lru_bx, lru_lambda,
           mlstm_norm, w_out, norm_mix, norm_ffn, peer_wq, peer_keys1, peer_keys2, peer_u, peer_v,
           norm_final):
    nb, tx, dm = x_prompt.shape
    ts = x_sample.shape[1]
    assert w_in.shape[0] == 1, "one layer"
    assert dm == D_MODEL and tx % CHUNK == 0 and CONV_W - 1 <= ts <= CHUNK
    n_x = tx // CHUNK
    tm = min(512, nb * 2 * CHUNK)
    assert (nb * tx) % tm == 0 and (nb * 2 * CHUNK) % tm == 0 and tm % LANES == 0
    te = 1024

    h_a = x_prompt.reshape(nb * tx, dm)
    meta = jnp.broadcast_to(meta_tokens[None].astype(F32), (nb, N_META, dm))
    h_t = jnp.concatenate([meta, jnp.zeros((nb, CHUNK - N_META, dm), F32),
                           x_sample, jnp.zeros((nb, CHUNK - ts, dm), F32)], axis=1)
    h_t = h_t.reshape(nb * 2 * CHUNK, dm)

    w0 = w_in[0]
    wmain = w0[:, :D_MAIN].astype(BF16)
    wg = jnp.pad(w0[:, D_MAIN:], ((0, 0), (0, LANES - 2 * HEADS)))
    wgh = wg.astype(BF16)
    wgm = (wg - wgh.astype(F32)).astype(BF16)
    bg = jnp.pad(b_gates[0], (0, LANES - 2 * HEADS))[None]
    row = lambda x: x.reshape(1, -1).astype(F32)

    xg_a, xg_t, q_a, q_t, g_a, g_t = _in_proj(h_a, h_t, row(norm_mix[0]), wmain, wgh, wgm, bg, tm)

    r3 = lambda x, rows: x.reshape(nb, rows, x.shape[-1])
    states = [jnp.pad(state_conv[0], ((0, 0), (SUBLANES - (CONV_W - 1), 0), (0, 0))),
              state_lru[0][:, None, :],
              state_mlstm_c[0].reshape(nb * HEADS, HEAD_DIM, HEAD_DIM),
              state_mlstm_n[0].reshape(nb * HEADS, 1, HEAD_DIM),
              jnp.broadcast_to(state_mlstm_m[0].reshape(nb * HEADS, 1, 1), (nb * HEADS, 1, LANES))]
    weights = [conv_w[0], row(conv_b[0]), _block_diag(lru_wa[0]).astype(BF16), row(lru_ba[0]),
               _block_diag(lru_wx[0]).astype(BF16), row(lru_bx[0]), row(lru_lambda[0]),
               row(mlstm_norm[0])]
    outs = _seq_mix(r3(xg_a, tx), r3(xg_t, 2 * CHUNK), r3(q_a, tx), r3(q_t, 2 * CHUNK),
                    r3(g_a, tx), r3(g_t, 2 * CHUNK), states, weights, nb, n_x, ts)
    y_a, y_t = outs[0], outs[1]
    st_p, st_s = outs[2:7], outs[7:12]

    wqt = peer_wq[0].T.astype(BF16)
    o_a, o_t = _peer_out(h_a, h_t, y_a.reshape(nb * tx, dm), y_t.reshape(nb * 2 * CHUNK, dm),
                         w_out[0].astype(BF16), row(norm_ffn[0]), wqt,
                         peer_keys1[0].astype(BF16), peer_keys2[0].astype(BF16),
                         peer_u[0].astype(BF16), peer_v[0].astype(BF16), row(norm_final), tm, te)

    y_prompt = o_a.reshape(nb, tx, dm)
    y_sample = o_t.reshape(nb, 2 * CHUNK, dm)[:, CHUNK:CHUNK + ts]

    def unpack(st):
        conv, lru, cm, nm, mm = st
        return (conv[:, SUBLANES - (CONV_W - 1):][None], lru[:, 0][None],
                cm.reshape(1, nb, HEADS, HEAD_DIM, HEAD_DIM), nm.reshape(1, nb, HEADS, HEAD_DIM),
                mm[:, 0, 0].reshape(1, nb, HEADS))

    return (y_prompt, y_sample) + unpack(st_p) + unpack(st_s)
```
